```python
import jax, jax.numpy as jnp
from jax import lax
import numpy as np

D_MODEL = 1024
BATCH = 8
SEQ = 4096
DEPTH = 2
DEC_BATCH = 32
DEC_SEQ = 4
PAST_LEN = 16384
PAGE_SIZE = 128

MIX_W = D_MODEL
CH_W = MIX_W // 4
CH_GROUPS = 4
CH_GROUP_DIM = CH_W // CH_GROUPS
CHUNK = 128
SB_W = MIX_W // 2
SB_HEAD_DIM = 64
SB_HEADS = SB_W // SB_HEAD_DIM
SB_QBLOCK = 128
SB_BIAS_INIT = -6.0
GLA_W = MIX_W - CH_W - SB_W
GLA_HEADS = 4
GLA_DK = GLA_W // GLA_HEADS
GLA_DV = GLA_W // GLA_HEADS
GLA_RANK = 16
GLA_TAU = 16.0
GLA_CHUNK = 64
D_FF = 4 * D_MODEL
EPS = 1e-6
IN_SPLITS = (CH_W, CH_W, SB_W, SB_W, SB_W, GLA_W, GLA_W, GLA_W, GLA_W, GLA_RANK)
IN_W = 2 * CH_W + 3 * SB_W + 4 * GLA_W + GLA_RANK

kernel_name = "hybrid_gmlp_stickbreak_gla_step"


def rms_norm(x, g):
    xf = x.astype(jnp.float32)
    y = xf * lax.rsqrt(jnp.mean(xf * xf, axis=-1, keepdims=True) + EPS)
    return (y * g.astype(jnp.float32)).astype(x.dtype)


def chunk_spatial_gate(u, v, g_v, w_s, b_s):
    bsz, t_len, _ = u.shape
    blk = CHUNK if t_len % CHUNK == 0 else t_len
    n_chunks = t_len // blk
    vg = rms_norm(v.reshape(bsz, t_len, CH_GROUPS, CH_GROUP_DIM), g_v)
    w = jnp.where(jnp.tril(jnp.ones((blk, blk), dtype=bool)), w_s[:, :blk, :blk], 0)
    mixed = jnp.einsum('gts,bnsgd->bntgd', w,
                       vg.reshape(bsz, n_chunks, blk, CH_GROUPS, CH_GROUP_DIM))
    mixed = mixed + b_s[:, :blk].T[None, None, :, :, None]
    y = u * mixed.reshape(bsz, t_len, CH_W)
    return y, vg.reshape(bsz, t_len, CH_W)


def stick_breaking_weights(z, causal):
    log_beta = jax.nn.log_sigmoid(z)
    log_rest = jnp.where(causal, log_beta - z, 0.0)
    between = lax.cumsum(log_rest, axis=z.ndim - 1, reverse=True) - log_rest
    return jnp.where(causal, jnp.exp(log_beta + between), 0.0)


def sb_attend_prompt(q, k, v, b_h):
    bsz, t_len, n_h, d = q.shape
    n_blk = t_len // SB_QBLOCK
    q_blocks = jnp.moveaxis(q.reshape(bsz, n_blk, SB_QBLOCK, n_h, d), 1, 0)
    k_pos = jnp.arange(t_len)
    scale = d ** -0.5
    bias = b_h.astype(jnp.float32)[None, :, None, None]

    def one_block(args):
        q_blk, i = args
        z = jnp.einsum('bqhd,bkhd->bhqk', q_blk, k, preferred_element_type=jnp.float32) * scale + bias
        q_pos = i * SB_QBLOCK + jnp.arange(SB_QBLOCK)
        a = stick_breaking_weights(z, k_pos[None, :] < q_pos[:, None])
        return jnp.einsum('bhqk,bkhd->bqhd', a.astype(v.dtype), v)

    out = lax.map(one_block, (q_blocks, jnp.arange(n_blk)))
    return jnp.moveaxis(out, 0, 1).reshape(bsz, t_len, n_h * d)


def sb_attend_sample(q, k, v, k_past, v_past, b_h):
    bsz, t_len, n_h, d = q.shape
    p_len = k_past.shape[1]
    scale = d ** -0.5
    bias = b_h.astype(jnp.float32)[None, :, None, None]
    z = jnp.concatenate([
        jnp.einsum('bqhd,bkhd->bhqk', q, k_past, preferred_element_type=jnp.float32),
        jnp.einsum('bqhd,bkhd->bhqk', q, k, preferred_element_type=jnp.float32)], axis=-1) * scale + bias
    q_pos = p_len + jnp.arange(t_len)
    k_pos = jnp.arange(p_len + t_len)
    a = stick_breaking_weights(z, k_pos[None, :] < q_pos[:, None]).astype(v.dtype)
    out = (jnp.einsum('bhqk,bkhd->bqhd', a[..., :p_len], v_past)
           + jnp.einsum('bhqk,bkhd->bqhd', a[..., p_len:], v))
    return out.reshape(bsz, t_len, n_h * d)


def gla_scan(q, k, v, log_a, s0):
    bsz, t_len, n_h, dk = q.shape
    c = GLA_CHUNK if t_len % GLA_CHUNK == 0 else t_len
    n = t_len // c

    def to_chunks(t):
        return jnp.moveaxis(t.astype(jnp.float32).reshape(bsz, n, c, *t.shape[2:]), 1, 0)

    qs, ks, vs, as_ = to_chunks(q * dk ** -0.5), to_chunks(k), to_chunks(v), to_chunks(log_a)
    incl = jnp.tril(jnp.ones((c, c), dtype=bool))[None, :, :, None, None]

    def step(s, inp):
        qc, kc, vc, ac = inp
        b = jnp.cumsum(ac, axis=1)
        b_last = b[:, -1]
        o_inter = jnp.einsum('bthk,bhkv->bthv', qc * jnp.exp(b), s)
        decay = jnp.exp(jnp.where(incl, b[:, :, None] - b[:, None, :], -jnp.inf))
        att = jnp.einsum('bthk,btshk,bshk->bhts', qc, decay, kc)
        o_intra = jnp.einsum('bhts,bshv->bthv', att, vc)
        s_new = (s * jnp.exp(b_last)[..., None]
                 + jnp.einsum('bshk,bshv->bhkv', kc * jnp.exp(b_last[:, None] - b), vc))
        return s_new, o_inter + o_intra

    s_final, o = lax.scan(step, s0.astype(jnp.float32), (qs, ks, vs, as_))
    o = jnp.moveaxis(o, 0, 1).reshape(bsz, t_len, n_h, -1)
    return o, s_final


def token_mix(h, w_in, g_chunk_v, w_spatial, b_spatial, g_q, g_k, b_sb, w_gate_a2, b_gate_a,
              g_gla_out, w_out, s0, kv_past):
    bsz, t_len, _ = h.shape
    parts = jnp.split(h @ w_in, np.cumsum(IN_SPLITS)[:-1].tolist(), axis=-1)
    c_u, c_v, s_q, s_k, s_v, l_q, l_k, l_v, l_g, l_a = parts
    y_a, v_rows = chunk_spatial_gate(jax.nn.gelu(c_u), jax.nn.gelu(c_v), g_chunk_v, w_spatial, b_spatial)
    q = rms_norm(s_q.reshape(bsz, t_len, SB_HEADS, SB_HEAD_DIM), g_q)
    k = rms_norm(s_k.reshape(bsz, t_len, SB_HEADS, SB_HEAD_DIM), g_k)
    vb = s_v.reshape(bsz, t_len, SB_HEADS, SB_HEAD_DIM)
    if kv_past is None:
        y_b = sb_attend_prompt(q, k, vb, b_sb)
    else:
        y_b = sb_attend_sample(q, k, vb, kv_past[0], kv_past[1], b_sb)
    log_a = jax.nn.log_sigmoid((l_a @ w_gate_a2 + b_gate_a).astype(jnp.float32)) / GLA_TAU
    o, s_new = gla_scan(l_q.reshape(bsz, t_len, GLA_HEADS, GLA_DK),
                        l_k.reshape(bsz, t_len, GLA_HEADS, GLA_DK),
                        l_v.reshape(bsz, t_len, GLA_HEADS, GLA_DV),
                        log_a.reshape(bsz, t_len, GLA_HEADS, GLA_DK), s0)
    y_c = rms_norm(o, g_gla_out).astype(h.dtype).reshape(bsz, t_len, GLA_W) * jax.nn.silu(l_g)
    y = jnp.concatenate([y_a, y_b, y_c], axis=-1) @ w_out
    return y, k, vb, s_new.astype(s0.dtype), v_rows


def sq_relu_ffn(h, w_up, w_down):
    return jnp.square(jax.nn.relu(h @ w_up)) @ w_down


def setup_inputs(seed: int = 0) -> dict:
    key = jax.random.key(seed)
    ks = jax.random.split(key, 24)
    n_pages = PAST_LEN // PAGE_SIZE
    n_used = DEC_BATCH * n_pages
    n_phys = n_used + n_used // 4

    def normal(k, shape, scale):
        return jax.random.normal(k, shape, jnp.float32) * scale

    def gain(k, shape):
        return 1.0 + normal(k, shape, 0.05)

    page_table = jax.random.permutation(ks[5], n_phys)[:n_used].reshape(DEC_BATCH, n_pages).astype(jnp.int32)
    return {
        "x_prompt": normal(ks[0], (BATCH, SEQ, D_MODEL), 1.0),
        "x_sample": normal(ks[1], (DEC_BATCH, DEC_SEQ, D_MODEL), 1.0),
        "cache_sb_k": normal(ks[2], (DEPTH, n_phys, PAGE_SIZE, SB_HEADS, SB_HEAD_DIM), 1.0),
        "cache_sb_v": normal(ks[3], (DEPTH, n_phys, PAGE_SIZE, SB_HEADS, SB_HEAD_DIM), 1.0),
        "state_gla": normal(ks[4], (DEPTH, DEC_BATCH, GLA_HEADS, GLA_DK, GLA_DV), 0.5),
        "page_table": page_table,
        "g_mix": gain(ks[6], (DEPTH, D_MODEL)),
        "w_in": normal(ks[7], (DEPTH, D_MODEL, IN_W), D_MODEL ** -0.5),
        "g_chunk_v": gain(ks[8], (DEPTH, CH_GROUPS, CH_GROUP_DIM)),
        "w_spatial": normal(ks[9], (DEPTH, CH_GROUPS, CHUNK, CHUNK), CHUNK ** -0.5),
        "b_spatial": 1.0 + normal(ks[10], (DEPTH, CH_GROUPS, CHUNK), 0.1),
        "g_q": gain(ks[11], (DEPTH, SB_HEAD_DIM)),
        "g_k": gain(ks[12], (DEPTH, SB_HEAD_DIM)),
        "b_sb": SB_BIAS_INIT + normal(ks[20], (DEPTH, SB_HEADS), 0.1),
        "w_gate_a2": normal(ks[13], (DEPTH, GLA_RANK, GLA_W), GLA_RANK ** -0.5),
        "b_gate_a": normal(ks[14], (DEPTH, GLA_W), 0.1),
        "g_gla_out": gain(ks[15], (DEPTH, GLA_DV)),
        "w_out": normal(ks[16], (DEPTH, MIX_W, D_MODEL), MIX_W ** -0.5),
        "g_ffn": gain(ks[17], (DEPTH, D_MODEL)),
        "w_up": normal(ks[18], (DEPTH, D_MODEL, D_FF), D_MODEL ** -0.5),
        "w_down": normal(ks[19], (DEPTH, D_FF, D_MODEL), D_FF ** -0.5),
    }


def reference(x_prompt, x_sample, cache_sb_k, cache_sb_v, state_gla, page_table, g_mix, w_in,
              g_chunk_v, w_spatial, b_spatial, g_q, g_k, b_sb, w_gate_a2, b_gate_a, g_gla_out, w_out,
              g_ffn, w_up, w_down):
    dec_b, n_pages = page_table.shape
    past = n_pages * PAGE_SIZE
    xp, xs = x_prompt, x_sample
    k_p, v_p, k_s, v_s, st_p, st_s, cv_s = [], [], [], [], [], [], []
    for l in range(DEPTH):
        lp = (w_in[l], g_chunk_v[l], w_spatial[l], b_spatial[l], g_q[l], g_k[l], b_sb[l],
              w_gate_a2[l], b_gate_a[l], g_gla_out[l], w_out[l])
        s0_p = jnp.zeros((xp.shape[0], GLA_HEADS, GLA_DK, GLA_DV), state_gla.dtype)
        yp, kp, vp, sp, _ = token_mix(rms_norm(xp, g_mix[l]), *lp, s0_p, None)
        xp = xp + yp
        xp = xp + sq_relu_ffn(rms_norm(xp, g_ffn[l]), w_up[l], w_down[l])
        k_past = cache_sb_k[l][page_table].reshape(dec_b, past, SB_HEADS, SB_HEAD_DIM)
        v_past = cache_sb_v[l][page_table].reshape(dec_b, past, SB_HEADS, SB_HEAD_DIM)
        ys, ksn, vsn, ss, cvs = token_mix(rms_norm(xs, g_mix[l]), *lp, state_gla[l], (k_past, v_past))
        xs = xs + ys
        xs = xs + sq_relu_ffn(rms_norm(xs, g_ffn[l]), w_up[l], w_down[l])
        k_p.append(kp); v_p.append(vp); st_p.append(sp)
        k_s.append(ksn); v_s.append(vsn); st_s.append(ss); cv_s.append(cvs)
    return (xp, xs, jnp.stack(k_p), jnp.stack(v_p), jnp.stack(k_s), jnp.stack(v_s),
            jnp.stack(st_p), jnp.stack(st_s), jnp.stack(cv_s))
```

```python
import functools

import numpy as np
import jax
import jax.numpy as jnp
from jax import lax
from jax.experimental import pallas as pl
from jax.experimental.pallas import tpu as pltpu

F32 = jnp.float32
BF = jnp.bfloat16

EPS = 1e-6
CH_GROUPS = 4
CHUNK = 128
SB_HEAD_DIM = 64
GLA_HEADS = 4
GLA_RANK = 16
GLA_TAU = 16.0
GLA_CHUNK = 64
GLA_SAMPLE_PAD = 16
SB_TILE = 256
SB_PAGES_PER_STEP = 8
V7X_VMEM_LIMIT = 56 * 1024 * 1024

_NT = (((1,), (1,)), ((), ()))
_TN = (((0,), (0,)), ((), ()))


def _dot(a, b):
    return jnp.dot(a, b, preferred_element_type=F32)


def _dot_nt(a, b):
    return lax.dot_general(a, b, _NT, preferred_element_type=F32)


def _split(x):
    hi = x.astype(BF)
    lo = (x - hi.astype(F32)).astype(BF)
    return hi, lo


def _split_dot(x, m):
    hi, lo = _split(x)
    return _dot(hi, m) + _dot(lo, m)


def _split_dot_left(m, x):
    hi, lo = _split(x)
    return _dot(m, hi) + _dot(m, lo)


def _rms(x, g_row):
    ms = jnp.mean(x * x, axis=-1, keepdims=True)
    return x * lax.rsqrt(ms + EPS) * g_row


def _gelu_tanh(x):
    c = np.float32(np.sqrt(2.0 / np.pi))
    return x * (0.5 * (1.0 + jnp.tanh(c * (x + 0.044715 * (x * x * x)))))


def _softplus(z):
    m = jnp.maximum(z, 0.0)
    return m + jnp.log(1.0 + jnp.exp(z - 2.0 * m))


def _const_spec(shape):
    nd = len(shape)
    return pl.BlockSpec(shape, lambda *_: (0,) * nd)


def _inproj_kernel(x_ref, gmix_ref, wcu_ref, wcv_ref, wq_ref, wk_ref, wv_ref, wlq_ref, wlk_ref,
                   wlv_ref, wlg_ref, wla_ref, w2_ref, bga_ref, gcv_ref, gq_ref, gk_ref, wmix_ref,
                   bmix_ref, bdc_ref, bds_ref, *out_refs, emit_kt, key_tile, sb_scale, gla_scale):
    if emit_kt:
        (ya_ref, q_ref, ktf_ref, vtf_ref, kt_ref, vb_ref,
         lq_ref, lk_ref, lv_ref, la_ref, gs_ref) = out_refs
    else:
        (ya_ref, vg_ref, q_ref, k_ref, v_ref, vb_ref,
         lq_ref, lk_ref, lv_ref, la_ref, gs_ref) = out_refs
    tm = x_ref.shape[0]
    h = _rms(x_ref[...], gmix_ref[...]).astype(BF)

    def proj(w_ref):
        return _dot(h, w_ref[...])

    u = _gelu_tanh(proj(wcu_ref))
    cv = _gelu_tanh(proj(wcv_ref))
    ch_w = cv.shape[1]
    gd = ch_w // CH_GROUPS
    ms = _split_dot(cv * cv, bdc_ref[...]) * (1.0 / gd)
    vg = cv * lax.rsqrt(ms + EPS) * gcv_ref[...]
    if not emit_kt:
        vg_ref[...] = vg
    ch_rows = wmix_ref.shape[1]
    lane_grp = lax.broadcasted_iota(jnp.int32, (ch_rows, ch_w), 1) // gd
    for c in range(tm // ch_rows):
        rows = slice(c * ch_rows, (c + 1) * ch_rows)
        vgb = vg[rows].astype(BF)
        mixed = jnp.zeros((ch_rows, ch_w), F32)
        for g in range(CH_GROUPS):
            mixed = jnp.where(lane_grp == g, _dot(wmix_ref[g], vgb), mixed)
        ya_ref[rows, :] = (u[rows] * (mixed + bmix_ref[...])).astype(ya_ref.dtype)

    hd = SB_HEAD_DIM
    sq = proj(wq_ref)
    msq = _split_dot(sq * sq, bds_ref[...]) * (1.0 / hd)
    q_ref[...] = (sq * lax.rsqrt(msq + EPS) * gq_ref[...] * sb_scale).astype(q_ref.dtype)
    sk = proj(wk_ref)
    msk = _split_dot(sk * sk, bds_ref[...]) * (1.0 / hd)
    k = sk * lax.rsqrt(msk + EPS) * gk_ref[...]
    v = proj(wv_ref)
    vb_ref[...] = v.astype(vb_ref.dtype)
    if emit_kt:
        kt = k.T
        ktf_ref[...] = kt
        vtf_ref[...] = v.T
        for c in range(tm // key_tile):
            kt_ref[c] = kt[:, c * key_tile:(c + 1) * key_tile].astype(kt_ref.dtype)
    else:
        k_ref[...] = k
        v_ref[...] = v

    lq_ref[...] = proj(wlq_ref) * gla_scale
    lk_ref[...] = proj(wlk_ref)
    lv_ref[...] = proj(wlv_ref)
    lg = proj(wlg_ref)
    gs_ref[...] = lg * (1.0 / (1.0 + jnp.exp(-lg)))
    la_low = proj(wla_ref).astype(BF)
    gate = _dot(la_low, w2_ref[...]) + bga_ref[...]
    la_ref[...] = (jnp.minimum(gate, 0.0) - jnp.log(1.0 + jnp.exp(-jnp.abs(gate)))) * (1.0 / GLA_TAU)


def _inproj_call(x, lw, wmix, bmix, *, tm, emit_kt, seq=None):
    n, d = x.shape
    ch_w = lw["wcu"].shape[1]
    sb_w = lw["wq"].shape[1]
    gla_w = lw["wlq"].shape[1]
    consts = [lw["gmix"], lw["wcu"], lw["wcv"], lw["wq"], lw["wk"], lw["wv"], lw["wlq"], lw["wlk"],
              lw["wlv"], lw["wlg"], lw["wla"], lw["w2"], lw["bga"], lw["gcv"], lw["gq"], lw["gk"],
              wmix, bmix, lw["bdc"], lw["bds"]]
    in_specs = [pl.BlockSpec((tm, d), lambda i: (i, 0))] + [_const_spec(c.shape) for c in consts]

    def row_out(w, dt):
        return jax.ShapeDtypeStruct((n, w), dt), pl.BlockSpec((tm, w), lambda i: (i, 0))

    if emit_kt:
        tiles = seq // tm
        transposed = (jax.ShapeDtypeStruct((n // seq, sb_w, seq), F32),
                      pl.BlockSpec((None, sb_w, tm), lambda i: (i // tiles, 0, i % tiles)))
        key_tiles = (jax.ShapeDtypeStruct((n // SB_TILE, sb_w, SB_TILE), BF),
                     pl.BlockSpec((tm // SB_TILE, sb_w, SB_TILE), lambda i: (i, 0, 0)))
        outs = [row_out(ch_w, BF), row_out(sb_w, BF), transposed, transposed, key_tiles,
                row_out(sb_w, BF)]
    else:
        outs = [row_out(ch_w, BF), row_out(ch_w, F32), row_out(sb_w, BF), row_out(sb_w, F32),
                row_out(sb_w, F32), row_out(sb_w, BF)]
    outs += [row_out(gla_w, F32) for _ in range(5)]
    kern = functools.partial(_inproj_kernel, emit_kt=emit_kt, key_tile=SB_TILE,
                             sb_scale=float(SB_HEAD_DIM) ** -0.5,
                             gla_scale=float(gla_w // GLA_HEADS) ** -0.5)
    return pl.pallas_call(
        kern,
        grid=(n // tm,),
        in_specs=in_specs,
        out_specs=[o[1] for o in outs],
        out_shape=[o[0] for o in outs],
        compiler_params=pltpu.CompilerParams(dimension_semantics=("parallel",),
                                             vmem_limit_bytes=V7X_VMEM_LIMIT),
        name="inproj_kt" if emit_kt else "inproj",
    )(x, *consts)


def _sb_tile(z, tri, carry, mask):
    sp = _softplus(z)
    lr = -sp
    if mask is not None:
        lr = jnp.where(mask, lr, 0.0)
    cum = _split_dot(lr, tri)
    a = jnp.exp(z - sp + cum + carry)
    if mask is not None:
        a = jnp.where(mask, a, 0.0)
    return a, carry + cum[:, :1] + lr[:, :1]


def _sbp_kernel(bias_ref, q_ref, kt_ref, v_ref, tri_ref, o_ref):
    hp = pl.program_id(1)
    qi = pl.program_id(2)
    tq, w = q_ref.shape
    tk = kt_ref.shape[2]
    hd = SB_HEAD_DIM
    lane_head = lax.broadcasted_iota(jnp.int32, (tq, w), 1) // hd
    row = lax.broadcasted_iota(jnp.int32, (tq, tk), 0)
    col = lax.broadcasted_iota(jnp.int32, (tq, tk), 1)
    causal = col < row
    q2 = q_ref[...]
    tri = tri_ref[...]
    out = jnp.zeros((tq, w), F32)
    for h in range(w // hd):
        bias = bias_ref[hp * (w // hd) + h]
        qh = jnp.where(lane_head == h, q2, jnp.zeros_like(q2))

        def tile(j, carry, acc, mask, qh=qh, bias=bias):
            z = _dot(qh, kt_ref[j]) + bias
            a, carry = _sb_tile(z, tri, carry, mask)
            off = pl.multiple_of(j * tk, tk)
            acc = acc + _dot(a.astype(BF), v_ref[pl.ds(off, tk), :])
            return carry, acc

        carry, acc = tile(qi, jnp.zeros((tq, 1), F32), jnp.zeros((tq, w), F32), causal)
        carry, acc = lax.fori_loop(
            0, qi, lambda i, c: tile(qi - 1 - i, c[0], c[1], None), (carry, acc))
        out = jnp.where(lane_head == h, acc, out)
    o_ref[...] = out.astype(o_ref.dtype)


def _sb_prompt_call(bias, q, kt, v, tri, *, batch, seq):
    n, sb_w = q.shape
    t = SB_TILE
    w = 2 * SB_HEAD_DIM
    nq = seq // t
    return pl.pallas_call(
        _sbp_kernel,
        grid=(batch, sb_w // w, nq),
        in_specs=[
            pl.BlockSpec(memory_space=pltpu.SMEM),
            pl.BlockSpec((t, w), lambda b, hp, qi: (b * nq + qi, hp)),
            pl.BlockSpec((nq, w, t), lambda b, hp, qi: (b, hp, 0)),
            pl.BlockSpec((seq, w), lambda b, hp, qi: (b, hp)),
            _const_spec(tri.shape),
        ],
        out_specs=pl.BlockSpec((t, w), lambda b, hp, qi: (b * nq + qi, hp)),
        out_shape=jax.ShapeDtypeStruct((n, sb_w), BF),
        compiler_params=pltpu.CompilerParams(
            dimension_semantics=("parallel", "parallel", "arbitrary"),
            vmem_limit_bytes=V7X_VMEM_LIMIT),
        name="sb_prompt",
    )(bias, q, kt, v, tri)


def _sbs_kernel(pt_ref, bias_ref, qbd_ref, knew_ref, vnew_ref, *rest, n_pages_step):
    k_refs = rest[:n_pages_step]
    v_refs = rest[n_pages_step:2 * n_pages_step]
    tri_ref, o_ref, acc_sc, carry_sc = rest[2 * n_pages_step:]
    del pt_ref
    s = pl.program_id(1)
    qbd = qbd_ref[...]
    rows, feat = qbd.shape
    keys = tri_ref.shape[0]
    n_heads = feat // SB_HEAD_DIM
    bias = bias_ref[...]
    tri = tri_ref[...]

    def block(kt, vt, mask):
        z = _dot(qbd, kt) + bias
        a, carry = _sb_tile(z, tri, carry_sc[...][:, :1], mask)
        carry_sc[...] = jnp.broadcast_to(carry, carry_sc.shape)
        acc_sc[...] += _dot_nt(a.astype(BF), vt)

    @pl.when(s == 0)
    def _():
        acc_sc[...] = jnp.zeros_like(acc_sc)
        carry_sc[...] = jnp.zeros_like(carry_sc)
        r = lax.broadcasted_iota(jnp.int32, (rows, keys), 0)
        j = lax.broadcasted_iota(jnp.int32, (rows, keys), 1)
        block(knew_ref[...], vnew_ref[...], j < r // n_heads)

    for p in range(n_pages_step):
        block(k_refs[p][...].astype(BF), v_refs[p][...].astype(BF), None)

    @pl.when(s == pl.num_programs(1) - 1)
    def _():
        r = lax.broadcasted_iota(jnp.int32, (rows, feat), 0)
        c = lax.broadcasted_iota(jnp.int32, (rows, feat), 1)
        own = jnp.where(c // SB_HEAD_DIM == r % n_heads, acc_sc[...], 0.0)
        o_ref[...] = own.reshape(rows // n_heads, n_heads, feat).sum(axis=1).astype(o_ref.dtype)


def _sb_sample_call(page_table, bias_rows, qbd, knew, vnew, cache_k, cache_v, tri, *, layer):
    dec_b, rows, feat = qbd.shape
    n_pages = page_table.shape[1]
    page = cache_k.shape[3]
    pp = SB_PAGES_PER_STEP
    n_steps = n_pages // pp
    t_len = rows // (feat // SB_HEAD_DIM)

    def page_spec(p):
        return pl.BlockSpec(
            (None, None, feat, page),
            lambda b, s, pt, p=p: (layer, pt[b, n_pages - 1 - (s * pp + p)], 0, 0))

    in_specs = [
        pl.BlockSpec(bias_rows.shape, lambda b, s, pt: (0, 0)),
        pl.BlockSpec((None, rows, feat), lambda b, s, pt: (b, 0, 0)),
        pl.BlockSpec((None, feat, page), lambda b, s, pt: (b, 0, 0)),
        pl.BlockSpec((None, feat, page), lambda b, s, pt: (b, 0, 0)),
    ] + [page_spec(p) for p in range(pp)] * 2 + [pl.BlockSpec(tri.shape, lambda b, s, pt: (0, 0))]
    grid_spec = pltpu.PrefetchScalarGridSpec(
        num_scalar_prefetch=1,
        grid=(dec_b, n_steps),
        in_specs=in_specs,
        out_specs=pl.BlockSpec((None, t_len, feat), lambda b, s, pt: (b, 0, 0)),
        scratch_shapes=[pltpu.VMEM((rows, feat), F32), pltpu.VMEM((rows, page), F32)],
    )
    return pl.pallas_call(
        functools.partial(_sbs_kernel, n_pages_step=pp),
        grid_spec=grid_spec,
        out_shape=jax.ShapeDtypeStruct((dec_b, t_len, feat), BF),
        compiler_params=pltpu.CompilerParams(
            dimension_semantics=("parallel", "arbitrary"),
            vmem_limit_bytes=V7X_VMEM_LIMIT),
        name="sb_sample",
    )(page_table, bias_rows, qbd, knew, vnew, *([cache_k] * pp), *([cache_v] * pp), tri)


def _gla_level_tables(c):
    n_levels = int(np.log2(c))
    t = np.arange(c)
    dq = np.zeros((n_levels, c, c), np.float32)
    dk = np.zeros((n_levels, c, c), np.float32)
    level = np.full((c, c), -1, np.int32)
    level[t, t] = n_levels
    for l in range(n_levels):
        m = c >> (l + 1)
        blk = t // m
        start = blk * m
        end = start + m - 1
        dq[l] = (t[None, :] >= start[:, None]) & (t[None, :] <= t[:, None])
        dk[l] = (t[None, :] > t[:, None]) & (t[None, :] <= end[:, None])
        pair = ((blk[:, None] % 2 == 1) & (blk[None, :] % 2 == 0)
                & (blk[:, None] // 2 == blk[None, :] // 2))
        level[pair] = l
    low = (t[None, :] <= t[:, None]).astype(np.float32)
    return low, dq.reshape(n_levels * c, c), dk.reshape(n_levels * c, c), level, n_levels


def _gla_kernel(q_ref, k_ref, v_ref, la_ref, gs_ref, s0_ref, low_ref, dq_ref, dk_ref, lv_ref,
                bd_ref, gout_ref, yc_ref, st_ref, st_sc, *, chunk, n_levels, zero_init):
    t_idx = pl.program_id(1)
    tt, w = q_ref.shape
    c = chunk
    n_heads = GLA_HEADS
    hd = w // n_heads

    @pl.when(t_idx == 0)
    def _():
        if zero_init:
            st_sc[...] = jnp.zeros_like(st_sc)
        else:
            st_sc[...] = s0_ref[...]

    lane_head = lax.broadcasted_iota(jnp.int32, (c, w), 1) // hd
    bd_mask = (lax.broadcasted_iota(jnp.int32, (w, w), 0) // hd
               == lax.broadcasted_iota(jnp.int32, (w, w), 1) // hd)
    low = low_ref[...]
    dq = dq_ref[...]
    dk = dk_ref[...]
    lv = lv_ref[...]
    bd = bd_ref[...]

    def stacked(x):
        xb = x.astype(BF)
        return jnp.concatenate(
            [jnp.where(lane_head == h, xb, jnp.zeros_like(xb)) for h in range(n_heads)], axis=0)

    def one_chunk(i, _):
        off = pl.multiple_of(i * c, c)
        rows = pl.ds(off, c)
        q = q_ref[rows, :]
        k = k_ref[rows, :]
        v = v_ref[rows, :]
        la = la_ref[rows, :]
        b = _split_dot_left(low, la)
        b_last = b[c - 1:c, :]
        st = st_sc[...]
        o = _dot_nt((q * jnp.exp(b)).astype(BF), st.astype(BF))
        eq = jnp.exp(_split_dot_left(dq, la))
        ek = jnp.exp(_split_dot_left(dk, la))
        att = _dot_nt(stacked(q), k.astype(BF))
        for l in range(n_levels):
            sl = slice(l * c, (l + 1) * c)
            a_l = _dot_nt(stacked(q * eq[sl]), (k * ek[sl]).astype(BF))
            att = jnp.where(lv == l, a_l, att)
        att = jnp.where(lv < 0, 0.0, att)
        r = _dot(att.astype(BF), v.astype(BF))
        for h in range(n_heads):
            o = o + jnp.where(lane_head == h, r[h * c:(h + 1) * c], 0.0)
        kd = k * jnp.exp(b_last - b)
        upd = lax.dot_general(v.astype(BF), kd.astype(BF), _TN, preferred_element_type=F32)
        st_sc[...] = st * jnp.exp(b_last) + jnp.where(bd_mask, upd, 0.0)
        ms = _split_dot(o * o, bd) * (1.0 / hd)
        y = o * lax.rsqrt(ms + EPS) * gout_ref[...]
        yc_ref[rows, :] = (y * gs_ref[rows, :]).astype(yc_ref.dtype)
        return 0

    lax.fori_loop(0, tt // c, one_chunk, 0)

    @pl.when(t_idx == pl.num_programs(1) - 1)
    def _():
        st_ref[...] = st_sc[...]


def _gla_call(q, k, v, la, gs, s0, gout, bd, *, chunk, tile, zero_init):
    bsz, t_len, w = q.shape
    low, dq, dk, level, n_levels = _gla_level_tables(chunk)
    lv = jnp.asarray(np.tile(level, (GLA_HEADS, 1)))
    low, dq, dk = (jnp.asarray(a, BF) for a in (low, dq, dk))
    seq_spec = pl.BlockSpec((None, tile, w), lambda b, t: (b, t, 0))
    st_spec = pl.BlockSpec((None, w, w), lambda b, t: (b, 0, 0))
    consts = [low, dq, dk, lv, bd, gout]
    return pl.pallas_call(
        functools.partial(_gla_kernel, chunk=chunk, n_levels=n_levels, zero_init=zero_init),
        grid=(bsz, t_len // tile),
        in_specs=[seq_spec] * 5 + [st_spec] + [pl.BlockSpec(a.shape, lambda b, t: (0, 0))
                                               for a in consts],
        out_specs=[seq_spec, st_spec],
        out_shape=[jax.ShapeDtypeStruct((bsz, t_len, w), BF),
                   jax.ShapeDtypeStruct((bsz, w, w), F32)],
        scratch_shapes=[pltpu.VMEM((w, w), F32)],
        compiler_params=pltpu.CompilerParams(
            dimension_semantics=("parallel", "arbitrary"),
            vmem_limit_bytes=V7X_VMEM_LIMIT),
        name="gla",
    )(q, k, v, la, gs, s0, *consts)


def _outffn_kernel(x_ref, ya_ref, yb_ref, yc_ref, woa_ref, wob_ref, woc_ref, gffn_ref, wup_ref,
                   wdn_ref, o_ref, *, ff_chunk):
    x1 = (x_ref[...] + _dot(ya_ref[...], woa_ref[...]) + _dot(yb_ref[...], wob_ref[...])
          + _dot(yc_ref[...], woc_ref[...]))
    hn = _rms(x1, gffn_ref[...]).astype(BF)
    o_ref[...] = x1
    for c in range(wup_ref.shape[1] // ff_chunk):
        cols = slice(c * ff_chunk, (c + 1) * ff_chunk)
        up = jnp.maximum(_dot(hn, wup_ref[:, cols]), 0.0)
        o_ref[...] += _dot((up * up).astype(BF), wdn_ref[cols, :])


def _outffn_call(x, ya, yb, yc, lw, *, tm):
    n, d = x.shape
    consts = [lw["woa"], lw["wob"], lw["woc"], lw["gffn"], lw["wup"], lw["wdn"]]

    def rows(a):
        return pl.BlockSpec((tm, a.shape[1]), lambda i: (i, 0))

    return pl.pallas_call(
        functools.partial(_outffn_kernel, ff_chunk=min(1024, lw["wup"].shape[1])),
        grid=(n // tm,),
        in_specs=[rows(x), rows(ya), rows(yb), rows(yc)] + [_const_spec(c.shape) for c in consts],
        out_specs=rows(x),
        out_shape=jax.ShapeDtypeStruct((n, d), F32),
        compiler_params=pltpu.CompilerParams(dimension_semantics=("parallel",),
                                             vmem_limit_bytes=V7X_VMEM_LIMIT),
        name="outffn",
    )(x, ya, yb, yc, *consts)


def _block_diag_ones(width, block):
    i = np.arange(width) // block
    return jnp.asarray(i[:, None] == i[None, :], BF)


def _strict_upper_sum_matrix(n):
    i = np.arange(n)
    return jnp.asarray(i[:, None] > i[None, :], BF)


def _layer_weights(l, g_mix, w_in, g_chunk_v, g_q, g_k, w_gate_a2, b_gate_a, g_gla_out, w_out,
                   g_ffn, w_up, w_down):
    d = w_in.shape[1]
    mix_w = w_out.shape[1]
    ch_w, sb_w = mix_w // 4, mix_w // 2
    gla_w = mix_w - ch_w - sb_w
    splits = np.cumsum([ch_w, ch_w, sb_w, sb_w, sb_w, gla_w, gla_w, gla_w, gla_w, GLA_RANK])
    names = ["wcu", "wcv", "wq", "wk", "wv", "wlq", "wlk", "wlv", "wlg", "wla"]
    wl = w_in[l].astype(BF)
    lw = {}
    lo = 0
    for name, hi in zip(names, splits):
        lw[name] = wl[:, lo:hi]
        lo = int(hi)
    n_sb_heads = sb_w // SB_HEAD_DIM
    rank_pad = 128 - GLA_RANK
    lw["wla"] = jnp.pad(lw["wla"], ((0, 0), (0, rank_pad)))
    lw["gmix"] = g_mix[l].reshape(1, d)
    lw["w2"] = jnp.pad(w_gate_a2[l].astype(BF), ((0, rank_pad), (0, 0)))
    lw["bga"] = b_gate_a[l].reshape(1, gla_w)
    lw["gcv"] = g_chunk_v[l].reshape(1, ch_w)
    lw["gq"] = jnp.tile(g_q[l], n_sb_heads).reshape(1, sb_w)
    lw["gk"] = jnp.tile(g_k[l], n_sb_heads).reshape(1, sb_w)
    lw["bdc"] = _block_diag_ones(ch_w, ch_w // CH_GROUPS)
    lw["bds"] = _block_diag_ones(sb_w, SB_HEAD_DIM)
    lw["bdg"] = _block_diag_ones(gla_w, gla_w // GLA_HEADS)
    lw["gout"] = jnp.tile(g_gla_out[l], GLA_HEADS).reshape(1, gla_w)
    wo = w_out[l].astype(BF)
    lw["woa"], lw["wob"], lw["woc"] = wo[:ch_w], wo[ch_w:ch_w + sb_w], wo[ch_w + sb_w:]
    lw["gffn"] = g_ffn[l].reshape(1, d)
    lw["wup"] = w_up[l].astype(BF)
    lw["wdn"] = w_down[l].astype(BF)
    return lw


def _spatial_tables(w_s, b_s, t_len, n_seq):
    if t_len % CHUNK == 0:
        blk, rep = CHUNK, 1
    else:
        blk, rep = t_len, n_seq
    tril = jnp.asarray(np.tril(np.ones((blk, blk), bool)))
    w = jnp.where(tril, w_s[:, :blk, :blk], 0.0)
    if rep > 1:
        eye = jnp.eye(rep, dtype=w.dtype)
        w = jnp.einsum("ab,gts->gatbs", eye, w).reshape(w.shape[0], rep * blk, rep * blk)
    bias_rows = jnp.tile(b_s[:, :blk].T, (rep, 1))
    return w.astype(BF), bias_rows


def kernel(x_prompt, x_sample, cache_sb_k, cache_sb_v, state_gla, page_table, g_mix, w_in, g_chunk_v, w_spatial, b_spatial, g_q, g_k, b_sb, w_gate_a2, b_gate_a, g_gla_out, w_out, g_ffn, w_up, w_down):
    bsz, seq, d = x_prompt.shape
    dec_b, dec_t, _ = x_sample.shape
    depth = w_in.shape[0]
    mix_w = w_out.shape[1]
    ch_w, sb_w = mix_w // 4, mix_w // 2
    gla_w = mix_w - ch_w - sb_w
    n_sb_heads = sb_w // SB_HEAD_DIM
    gd = ch_w // CH_GROUPS
    n_phys, page = cache_sb_k.shape[1], cache_sb_k.shape[2]
    assert seq % SB_TILE == 0 and seq % CHUNK == 0 and seq % GLA_CHUNK == 0
    assert dec_t < GLA_SAMPLE_PAD and dec_t < page

    to_pages_t = lambda c: jnp.transpose(c, (0, 1, 3, 4, 2)).reshape(depth, n_phys, sb_w, page)
    cache_k, cache_v = to_pages_t(cache_sb_k), to_pages_t(cache_sb_v)
    from_rows_t = lambda a: jnp.transpose(
        a.reshape(bsz, n_sb_heads, SB_HEAD_DIM, seq), (0, 3, 1, 2))
    tri_p = _strict_upper_sum_matrix(SB_TILE)
    tri_s = _strict_upper_sum_matrix(page)
    n_p, n_s = bsz * seq, dec_b * dec_t
    tm_p = 256 if n_p % 256 == 0 else SB_TILE
    gla_tile = min(seq, 512)
    head_of_col = np.arange(sb_w) // SB_HEAD_DIM
    qbd_mask = jnp.asarray(head_of_col[None, :] == np.arange(n_sb_heads)[:, None], BF)

    xp = x_prompt.reshape(n_p, d)
    xs = x_sample.reshape(n_s, d)
    outs = {k: [] for k in ("kp", "vp", "ks", "vs", "stp", "sts", "cvs")}

    def unpack_state(st, n):
        st = st.reshape(n, GLA_HEADS, gla_w // GLA_HEADS, GLA_HEADS, gla_w // GLA_HEADS)
        st = jnp.stack([st[:, h, :, h, :] for h in range(GLA_HEADS)], axis=1)
        return jnp.swapaxes(st, -1, -2)

    for l in range(depth):
        lw = _layer_weights(l, g_mix, w_in, g_chunk_v, g_q, g_k, w_gate_a2, b_gate_a, g_gla_out,
                            w_out, g_ffn, w_up, w_down)
        wmix_p, brow_p = _spatial_tables(w_spatial[l], b_spatial[l], seq, bsz)
        wmix_s, brow_s = _spatial_tables(w_spatial[l], b_spatial[l], dec_t, dec_b)
        bmix_p = jnp.repeat(brow_p, gd, axis=1)
        bmix_s = jnp.repeat(brow_s, gd, axis=1)

        (ya, q, ktf, vtf, kt, vb, lq, lk, lv, la, gs) = _inproj_call(
            xp, lw, wmix_p, bmix_p, tm=tm_p, emit_kt=True, seq=seq)
        yb = _sb_prompt_call(b_sb[l], q, kt, vb, tri_p, batch=bsz, seq=seq)
        r3 = lambda a: a.reshape(bsz, seq, gla_w)
        yc, st = _gla_call(r3(lq), r3(lk), r3(lv), r3(la), r3(gs),
                           jnp.zeros((bsz, gla_w, gla_w), F32), lw["gout"], lw["bdg"],
                           chunk=GLA_CHUNK, tile=gla_tile, zero_init=True)
        xp = _outffn_call(xp, ya, yb, yc.reshape(n_p, gla_w), lw, tm=tm_p)
        outs["kp"].append(from_rows_t(ktf))
        outs["vp"].append(from_rows_t(vtf))
        outs["stp"].append(unpack_state(st, bsz))

        (ya, vg, q, k, v, vb, lq, lk, lv, la, gs) = _inproj_call(
            xs, lw, wmix_s, bmix_s, tm=n_s, emit_kt=False)
        qbd = (q.reshape(dec_b, dec_t, 1, sb_w) * qbd_mask[None, None]).reshape(
            dec_b, dec_t * n_sb_heads, sb_w)
        bias_rows = jnp.broadcast_to(jnp.tile(b_sb[l], dec_t)[:, None], (dec_t * n_sb_heads, page))
        pad_keys = lambda a: jnp.pad(jnp.swapaxes(a.reshape(dec_b, dec_t, sb_w), 1, 2),
                                     ((0, 0), (0, 0), (0, page - dec_t)))
        yb = _sb_sample_call(page_table, bias_rows, qbd, pad_keys(k.astype(BF)), pad_keys(vb),
                             cache_k, cache_v, tri_s, layer=l)
        pad_t = lambda a: jnp.pad(a.reshape(dec_b, dec_t, gla_w),
                                  ((0, 0), (0, GLA_SAMPLE_PAD - dec_t), (0, 0)))
        s0 = jnp.swapaxes(state_gla[l], -1, -2)
        eye = jnp.eye(GLA_HEADS, dtype=F32)
        s0 = jnp.einsum("bhvk,hg->bhvgk", s0, eye).reshape(dec_b, gla_w, gla_w)
        yc, st = _gla_call(pad_t(lq), pad_t(lk), pad_t(lv), pad_t(la), pad_t(gs), s0,
                           lw["gout"], lw["bdg"], chunk=GLA_SAMPLE_PAD, tile=GLA_SAMPLE_PAD,
                           zero_init=False)
        xs = _outffn_call(xs, ya, yb.reshape(n_s, sb_w), yc[:, :dec_t].reshape(n_s, gla_w), lw,
                          tm=n_s)
        outs["ks"].append(k.reshape(dec_b, dec_t, n_sb_heads, SB_HEAD_DIM))
        outs["vs"].append(v.reshape(dec_b, dec_t, n_sb_heads, SB_HEAD_DIM))
        outs["sts"].append(unpack_state(st, dec_b))
        outs["cvs"].append(vg.reshape(dec_b, dec_t, ch_w))

    stack = lambda key: jnp.stack(outs[key])
    return (xp.reshape(bsz, seq, d), xs.reshape(dec_b, dec_t, d), stack("kp"), stack("vp"),
            stack("ks"), stack("vs"), stack("stp"), stack("sts"), stack("cvs"))
```
